```python
import jax, jax.numpy as jnp
from jax import lax
import numpy as np

D_MODEL = 1024
BATCH = 8
SEQ = 8192
DEPTH = 4

MEM_LEN = 256
D_MIX = D_MODEL
DN_HEADS = 4
DN_HEAD_DIM = D_MODEL // 8
DN_DIM = DN_HEADS * DN_HEAD_DIM
CONV_WIDTH = 4
GLA_HEADS = 4
GLA_HEAD_V = D_MODEL // 8
GLA_HEAD_K = GLA_HEAD_V // 2
GLA_V_DIM = GLA_HEADS * GLA_HEAD_V
GLA_K_DIM = GLA_HEADS * GLA_HEAD_K
GLA_LOWRANK = 16
GLA_TAU = 16.0
CHUNK = 64
XA_HEADS = 4
XA_HEAD_DIM = D_MODEL // XA_HEADS
D_FF = 4 * D_MODEL
EPS = 1e-6

IN_SIZES = (DN_DIM, DN_DIM, DN_DIM, DN_DIM, DN_HEADS, DN_HEADS,
            GLA_K_DIM, GLA_K_DIM, GLA_V_DIM, GLA_V_DIM, GLA_LOWRANK)
IN_OFFSETS = tuple(int(o) for o in np.cumsum(IN_SIZES)[:-1])
N_IN = int(sum(IN_SIZES))

kernel_name = "hymba_style_deltanet_gla_hybrid"


def rmsnorm(x, w):
    xf = x.astype(jnp.float32)
    y = xf * lax.rsqrt(jnp.mean(xf * xf, axis=-1, keepdims=True) + EPS)
    return (y * w.astype(jnp.float32)).astype(x.dtype)


def l2norm(t):
    return t * lax.rsqrt(jnp.sum(t * t, axis=-1, keepdims=True) + EPS)


def causal_conv(x, w):
    c = x.shape[-1]
    return lax.conv_general_dilated(
        x, w[:, None, :], window_strides=(1,), padding=[(CONV_WIDTH - 1, 0)],
        dimension_numbers=("NWC", "WIO", "NWC"), feature_group_count=c)


def to_chunks(t):
    b, s, h = t.shape[:3]
    t = t.reshape(b, s // CHUNK, CHUNK, h, *t.shape[3:])
    return t.transpose((1, 0, 3, 2) + tuple(range(4, t.ndim)))


def from_chunks(o):
    n, b, h, c, d = o.shape
    return o.transpose(1, 0, 3, 2, 4).reshape(b, n * c, h, d)


def gated_delta_rule(q, k, v, g, beta):
    b, _, h, dk = q.shape
    dv = v.shape[-1]
    q, k, v, g, beta = (to_chunks(t) for t in (q, k, v, g, beta))
    gc = jnp.cumsum(g, axis=-1)
    causal = jnp.tril(jnp.ones((CHUNK, CHUNK), dtype=bool))
    strict = jnp.tril(jnp.ones((CHUNK, CHUNK), dtype=bool), k=-1)
    decay = jnp.exp(jnp.where(causal, gc[..., :, None] - gc[..., None, :], -jnp.inf))
    kk = jnp.einsum("nbhik,nbhjk->nbhij", k, k)
    a_strict = jnp.where(strict, beta[..., :, None] * kk * decay, 0.0)
    lhs = jnp.eye(CHUNK, dtype=jnp.float32) + a_strict
    w_mat = lax.linalg.triangular_solve(lhs, (beta * jnp.exp(gc))[..., None] * k,
                                        left_side=True, lower=True, unit_diagonal=True)
    u_mat = lax.linalg.triangular_solve(lhs, beta[..., None] * v,
                                        left_side=True, lower=True, unit_diagonal=True)
    a_qk = jnp.einsum("nbhik,nbhjk->nbhij", q, k) * decay
    q_g = q * jnp.exp(gc)[..., None]
    k_d = k * jnp.exp(gc[..., -1:] - gc)[..., None]
    g_last = jnp.exp(gc[..., -1])

    def step(state, inp):
        w_c, u_c, aqk_c, qg_c, kd_c, gl_c = inp
        v_new = u_c - jnp.einsum("bhck,bhkv->bhcv", w_c, state)
        o = jnp.einsum("bhck,bhkv->bhcv", qg_c, state) + jnp.einsum("bhij,bhjv->bhiv", aqk_c, v_new)
        state = state * gl_c[..., None, None] + jnp.einsum("bhck,bhcv->bhkv", kd_c, v_new)
        return state, o

    s0 = jnp.zeros((b, h, dk, dv), jnp.float32)
    _, o = lax.scan(step, s0, (w_mat, u_mat, a_qk, q_g, k_d, g_last))
    return from_chunks(o)


def gla_chunked(q, k, v, log_a):
    b, _, h, dk = q.shape
    dv = v.shape[-1]
    q, k, v, log_a = (to_chunks(t) for t in (q, k, v, log_a))
    gc = jnp.cumsum(log_a, axis=-2)
    causal = jnp.tril(jnp.ones((CHUNK, CHUNK), dtype=bool))[:, :, None]

    def step(state, inp):
        q_c, k_c, v_c, gc_c = inp
        diff = gc_c[:, :, :, None, :] - gc_c[:, :, None, :, :]
        decay = jnp.exp(jnp.where(causal, diff, -jnp.inf))
        att = jnp.sum(q_c[:, :, :, None, :] * k_c[:, :, None, :, :] * decay, axis=-1)
        o = (jnp.einsum("bhck,bhkv->bhcv", q_c * jnp.exp(gc_c), state)
             + jnp.einsum("bhij,bhjv->bhiv", att, v_c))
        g_last = gc_c[:, :, -1:, :]
        state = (state * jnp.exp(g_last)[:, :, 0, :, None]
                 + jnp.einsum("bhck,bhcv->bhkv", k_c * jnp.exp(g_last - gc_c), v_c))
        return state, o

    s0 = jnp.zeros((b, h, dk, dv), jnp.float32)
    _, o = lax.scan(step, s0, (q, k, v, gc))
    return from_chunks(o)


def hybrid_mixer(xn, w_in, conv_w, dn_a_log, dn_dt_bias, dn_norm_w,
                 gla_w_gate2, gla_b_gate, gla_norm_w, w_out):
    b, s, _ = xn.shape
    f32 = jnp.float32
    proj = xn @ w_in
    dq, dk_, dv, dz, db, da, gq, gk, gv, gr, glr = jnp.split(proj, IN_OFFSETS, axis=-1)

    def heads(t, h):
        return t.reshape(b, s, h, -1).astype(f32)

    qkv = jax.nn.silu(causal_conv(jnp.concatenate([dq, dk_, dv], axis=-1), conv_w))
    cq, ck, cv = jnp.split(qkv, 3, axis=-1)
    q_dn = l2norm(heads(cq, DN_HEADS)) * (DN_HEAD_DIM ** -0.5)
    k_dn = l2norm(heads(ck, DN_HEADS))
    v_dn = heads(cv, DN_HEADS)
    beta = jax.nn.sigmoid(db.astype(f32))
    g_dn = -jnp.exp(dn_a_log.astype(f32)) * jax.nn.softplus(da.astype(f32) + dn_dt_bias.astype(f32))
    o_dn = gated_delta_rule(q_dn, k_dn, v_dn, g_dn, beta)
    o_dn = rmsnorm(o_dn, dn_norm_w) * jax.nn.silu(heads(dz, DN_HEADS))

    gate_logit = (glr @ gla_w_gate2).astype(f32) + gla_b_gate.astype(f32)
    log_a = heads(jax.nn.log_sigmoid(gate_logit) / GLA_TAU, GLA_HEADS)
    q_g = heads(gq, GLA_HEADS) * (GLA_HEAD_K ** -0.5)
    o_gla = gla_chunked(q_g, heads(gk, GLA_HEADS), heads(gv, GLA_HEADS), log_a)
    o_gla = rmsnorm(o_gla, gla_norm_w) * jax.nn.silu(heads(gr, GLA_HEADS))

    o = jnp.concatenate([o_dn.reshape(b, s, DN_DIM), o_gla.reshape(b, s, GLA_V_DIM)], axis=-1)
    return o.astype(xn.dtype) @ w_out


def memory_cross_attention(xn, memn, wq, wk, wv, wo):
    b, s, _ = xn.shape
    m = memn.shape[1]
    q = (xn @ wq).reshape(b, s, XA_HEADS, XA_HEAD_DIM)
    k = (memn @ wk).reshape(b, m, XA_HEADS, XA_HEAD_DIM)
    v = (memn @ wv).reshape(b, m, XA_HEADS, XA_HEAD_DIM)
    scores = jnp.einsum("bshd,bmhd->bhsm", q, k).astype(jnp.float32) * (XA_HEAD_DIM ** -0.5)
    p = jax.nn.softmax(scores, axis=-1).astype(v.dtype)
    o = jnp.einsum("bhsm,bmhd->bshd", p, v).reshape(b, s, D_MODEL)
    return o @ wo


def sqrelu_mlp(xn, w1, w2):
    return jnp.square(jax.nn.relu(xn @ w1)) @ w2


def setup_inputs(seed: int = 0) -> dict:
    key = jax.random.key(seed)
    ks = jax.random.split(key, 24)
    L = DEPTH
    nrm = jax.random.normal

    def gain(k, shape):
        return 1.0 + 0.02 * nrm(k, shape, jnp.float32)

    dt = jnp.exp(jax.random.uniform(ks[5], (L, DN_HEADS), jnp.float32,
                                    float(np.log(1e-3)), float(np.log(1e-1))))
    return {
        "x": nrm(ks[0], (BATCH, SEQ, D_MODEL), jnp.float32),
        "mem": nrm(ks[1], (BATCH, MEM_LEN, D_MODEL), jnp.float32),
        "mix_norm_w": gain(ks[2], (L, D_MODEL)),
        "w_in": nrm(ks[3], (L, D_MODEL, N_IN), jnp.float32) * D_MODEL ** -0.5,
        "conv_w": nrm(ks[4], (L, CONV_WIDTH, 3 * DN_DIM), jnp.float32) * CONV_WIDTH ** -0.5,
        "dn_a_log": jnp.log(jax.random.uniform(ks[6], (L, DN_HEADS), jnp.float32, 1.0, 16.0)),
        "dn_dt_bias": dt + jnp.log(-jnp.expm1(-dt)),
        "dn_norm_w": gain(ks[7], (L, DN_HEAD_DIM)),
        "gla_w_gate2": nrm(ks[8], (L, GLA_LOWRANK, GLA_K_DIM), jnp.float32) * GLA_LOWRANK ** -0.5,
        "gla_b_gate": 0.1 * nrm(ks[9], (L, GLA_K_DIM), jnp.float32),
        "gla_norm_w": gain(ks[10], (L, GLA_HEAD_V)),
        "w_out": nrm(ks[11], (L, D_MIX, D_MODEL), jnp.float32) * D_MIX ** -0.5,
        "xa_norm_w": gain(ks[12], (L, D_MODEL)),
        "mem_norm_w": gain(ks[13], (L, D_MODEL)),
        "xa_wq": nrm(ks[14], (L, D_MODEL, D_MODEL), jnp.float32) * D_MODEL ** -0.5,
        "xa_wk": nrm(ks[15], (L, D_MODEL, D_MODEL), jnp.float32) * D_MODEL ** -0.5,
        "xa_wv": nrm(ks[16], (L, D_MODEL, D_MODEL), jnp.float32) * D_MODEL ** -0.5,
        "xa_wo": nrm(ks[17], (L, D_MODEL, D_MODEL), jnp.float32) * D_MODEL ** -0.5,
        "mlp_norm_w": gain(ks[18], (L, D_MODEL)),
        "mlp_w1": nrm(ks[19], (L, D_MODEL, D_FF), jnp.float32) * D_MODEL ** -0.5,
        "mlp_w2": nrm(ks[20], (L, D_FF, D_MODEL), jnp.float32) * D_FF ** -0.5,
        "final_norm_w": gain(ks[21], (D_MODEL,)),
    }


def reference(x, mem, mix_norm_w, w_in, conv_w, dn_a_log, dn_dt_bias, dn_norm_w,
              gla_w_gate2, gla_b_gate, gla_norm_w, w_out, xa_norm_w, mem_norm_w,
              xa_wq, xa_wk, xa_wv, xa_wo, mlp_norm_w, mlp_w1, mlp_w2, final_norm_w):
    h = x
    for l in range(DEPTH):
        h = h + hybrid_mixer(rmsnorm(h, mix_norm_w[l]), w_in[l], conv_w[l], dn_a_log[l],
                             dn_dt_bias[l], dn_norm_w[l], gla_w_gate2[l], gla_b_gate[l],
                             gla_norm_w[l], w_out[l])
        h = h + memory_cross_attention(rmsnorm(h, xa_norm_w[l]), rmsnorm(mem, mem_norm_w[l]),
                                       xa_wq[l], xa_wk[l], xa_wv[l], xa_wo[l])
        h = h + sqrelu_mlp(rmsnorm(h, mlp_norm_w[l]), mlp_w1[l], mlp_w2[l])
    return rmsnorm(h, final_norm_w)
```

```python
import functools

import jax
import jax.numpy as jnp
import numpy as np
from jax import lax
from jax.experimental import pallas as pl
from jax.experimental.pallas import tpu as pltpu

F32 = jnp.float32
BF16 = jnp.bfloat16

EPS = 1e-6
CHUNK = 64
CONV_WIDTH = 4
DN_HEADS = 4
DN_HEAD_DIM = 128
DN_DIM = DN_HEADS * DN_HEAD_DIM
GLA_HEADS = 4
GLA_HEAD_K = 64
GLA_HEAD_V = 128
GLA_K_DIM = GLA_HEADS * GLA_HEAD_K
GLA_V_DIM = GLA_HEADS * GLA_HEAD_V
GLA_LOWRANK = 16
GLA_TAU = 16.0
XA_HEADS = 4
LANES = 128
SMALL_W = LANES
VMEM_LIMIT = 56 * 1024 * 1024

ROW_TILE = 512
MIX_TILE = 512
FF_TILE = 1024

GLA_LEVELS = (32, 16, 8, 4, 2, 1)


def _mm(a, b):
    return jnp.dot(a, b, preferred_element_type=F32)


def _mm_nt(a, b):
    return lax.dot_general(a, b, (((1,), (1,)), ((), ())), preferred_element_type=F32)


def _mm_tn(a, b):
    return lax.dot_general(a, b, (((0,), (0,)), ((), ())), preferred_element_type=F32)


def _rms(x, w):
    return x * lax.rsqrt(jnp.mean(x * x, axis=-1, keepdims=True) + EPS) * w


def _softplus(x):
    return jnp.maximum(x, 0.0) + jnp.log1p(jnp.exp(-jnp.abs(x)))


def _silu(x):
    return x * jax.nn.sigmoid(x)


def _proj_kernel(h_ref, nw_ref, wdn_ref, wgla_ref, wsm_ref, wg2_ref, gpar_ref, bg_ref,
                 dn_ref, gla_ref, gates_ref, loga_ref):
    xb = _rms(h_ref[...], nw_ref[...]).astype(BF16)
    dn_ref[...] = _mm(xb, wdn_ref[...]).astype(BF16)
    gla_ref[...] = _mm(xb, wgla_ref[...]).astype(BF16)
    sm = _mm(xb, wsm_ref[...])
    lane = lax.broadcasted_iota(jnp.int32, sm.shape, 1)
    beta = jax.nn.sigmoid(sm)
    g = -jnp.exp(gpar_ref[0:1, :]) * _softplus(sm + gpar_ref[1:2, :])
    gates_ref[...] = jnp.where(lane < DN_HEADS, beta, g)
    logit = _mm(sm.astype(BF16), wg2_ref[...]) + bg_ref[...]
    loga_ref[...] = -_softplus(-logit) * (1.0 / GLA_TAU)


def _proj_call(h, nw, wdn, wgla, wsm, wg2, gpar, bg, layer):
    t, d = h.shape
    tm = ROW_TILE
    wspec = lambda shape: pl.BlockSpec((None,) + shape, lambda i: (layer, 0, 0))
    return pl.pallas_call(
        _proj_kernel,
        grid=(t // tm,),
        in_specs=[
            pl.BlockSpec((tm, d), lambda i: (i, 0)),
            wspec((1, d)),
            wspec((d, 4 * DN_DIM)),
            wspec((d, 2 * GLA_K_DIM + 2 * GLA_V_DIM)),
            wspec((d, SMALL_W)),
            wspec((SMALL_W, GLA_K_DIM)),
            wspec((8, SMALL_W)),
            wspec((1, GLA_K_DIM)),
        ],
        out_specs=[
            pl.BlockSpec((tm, 4 * DN_DIM), lambda i: (i, 0)),
            pl.BlockSpec((tm, 2 * GLA_K_DIM + 2 * GLA_V_DIM), lambda i: (i, 0)),
            pl.BlockSpec((tm, SMALL_W), lambda i: (i, 0)),
            pl.BlockSpec((tm, GLA_K_DIM), lambda i: (i, 0)),
        ],
        out_shape=[
            jax.ShapeDtypeStruct((t, 4 * DN_DIM), BF16),
            jax.ShapeDtypeStruct((t, 2 * GLA_K_DIM + 2 * GLA_V_DIM), BF16),
            jax.ShapeDtypeStruct((t, SMALL_W), F32),
            jax.ShapeDtypeStruct((t, GLA_K_DIM), F32),
        ],
        compiler_params=pltpu.CompilerParams(
            dimension_semantics=("arbitrary",), vmem_limit_bytes=VMEM_LIMIT),
    )(h, nw, wdn, wgla, wsm, wg2, gpar, bg)


def _dn_kernel(x_ref, gates_ref, cw_ref, nw_ref, tri_ref, o_ref, xpad_ref, state_ref, *, tb):
    c3 = 3 * DN_DIM
    hd = DN_HEAD_DIM

    @pl.when(pl.program_id(1) == 0)
    def _():
        state_ref[...] = jnp.zeros_like(state_ref)
        xpad_ref[0:8, :] = jnp.zeros((8, c3), F32)

    xpad_ref[8:8 + tb, :] = x_ref[:, 0:c3].astype(F32)

    row = lax.broadcasted_iota(jnp.int32, (CHUNK, CHUNK), 0)
    col = lax.broadcasted_iota(jnp.int32, (CHUNK, CHUNK), 1)
    causal = row >= col
    strict = row > col
    eye = (row == col).astype(F32)

    def chunk(c, carry):
        base = pl.multiple_of(c * CHUNK, CHUNK)

        def conv(col):
            win = xpad_ref[pl.ds(base, CHUNK + 8), col:col + hd]
            y = win[8:] * cw_ref[3:4, col:col + hd]
            for w in range(CONV_WIDTH - 1):
                shifted = pltpu.roll(win, CONV_WIDTH - 1 - w, 0)
                y = y + shifted[8:] * cw_ref[w:w + 1, col:col + hd]
            return _silu(y)

        gt = gates_ref[pl.ds(base, CHUNK), :]
        gc = jnp.dot(tri_ref[...], gt, precision=lax.Precision.HIGHEST,
                     preferred_element_type=F32)
        gct = gc.T
        z_all = x_ref[pl.ds(base, CHUNK), c3:c3 + DN_DIM].astype(F32)
        outs = []
        for h in range(DN_HEADS):
            q = conv(h * hd)
            k = conv(DN_DIM + h * hd)
            v = conv(2 * DN_DIM + h * hd)
            q = q * lax.rsqrt(jnp.sum(q * q, axis=-1, keepdims=True) + EPS) * (hd ** -0.5)
            k = k * lax.rsqrt(jnp.sum(k * k, axis=-1, keepdims=True) + EPS)
            gcol = gc[:, DN_HEADS + h:DN_HEADS + h + 1]
            grow = gct[DN_HEADS + h:DN_HEADS + h + 1, :]
            bcol = gt[:, h:h + 1]
            glast = gc[CHUNK - 1:CHUNK, DN_HEADS + h:DN_HEADS + h + 1]
            decay = jnp.exp(jnp.where(causal, gcol - grow, -jnp.inf))
            kb = k.astype(BF16)
            kk = _mm_nt(kb, kb)
            aqk = _mm_nt(q.astype(BF16), kb) * decay
            a = jnp.where(strict, bcol * kk * decay, 0.0)
            tm = -a
            ab = a.astype(BF16)
            xf = _mm(ab, ab)
            for it in range(5):
                xb = xf.astype(BF16)
                tm = tm + xf + _mm(tm.astype(BF16), xb)
                if it < 4:
                    xf = _mm(xb, xb)
            egc = jnp.exp(gcol)
            rhs = jnp.concatenate([k * (bcol * egc), v * bcol], axis=1)
            wu = rhs + _mm(tm.astype(BF16), rhs.astype(BF16))
            w_c = wu[:, 0:hd]
            u_c = wu[:, hd:2 * hd]
            qg = q * egc
            kd = k * jnp.exp(glast - gcol)
            s_old = state_ref[h]
            r1 = _mm(jnp.concatenate([qg, w_c], axis=0).astype(BF16), s_old.astype(BF16))
            v_new = u_c - r1[CHUNK:2 * CHUNK]
            vb = v_new.astype(BF16)
            o = r1[0:CHUNK] + _mm(aqk.astype(BF16), vb)
            state_ref[h] = s_old * jnp.exp(glast) + _mm_tn(kd.astype(BF16), vb)
            z = z_all[:, h * hd:(h + 1) * hd]
            outs.append(_rms(o, nw_ref[...]) * _silu(z))
        o_ref[pl.ds(base, CHUNK), :] = jnp.concatenate(outs, axis=1).astype(BF16)
        return carry

    lax.fori_loop(0, tb // CHUNK, chunk, 0)
    xpad_ref[0:8, :] = xpad_ref[tb:tb + 8, :]


def _dn_call(dn_in, gates, cw, nw, tri, layer, batch):
    t = dn_in.shape[0]
    tb = MIX_TILE
    nb = t // batch // tb
    return pl.pallas_call(
        functools.partial(_dn_kernel, tb=tb),
        grid=(batch, nb),
        in_specs=[
            pl.BlockSpec((tb, 4 * DN_DIM), lambda b, j: (b * nb + j, 0)),
            pl.BlockSpec((tb, SMALL_W), lambda b, j: (b * nb + j, 0)),
            pl.BlockSpec((None, 8, 3 * DN_DIM), lambda b, j: (layer, 0, 0)),
            pl.BlockSpec((None, 1, DN_HEAD_DIM), lambda b, j: (layer, 0, 0)),
            pl.BlockSpec((CHUNK, CHUNK), lambda b, j: (0, 0)),
        ],
        out_specs=pl.BlockSpec((tb, DN_DIM), lambda b, j: (b * nb + j, 0)),
        out_shape=jax.ShapeDtypeStruct((t, DN_DIM), BF16),
        scratch_shapes=[
            pltpu.VMEM((tb + 8, 3 * DN_DIM), F32),
            pltpu.VMEM((DN_HEADS, DN_HEAD_DIM, DN_HEAD_DIM), F32),
        ],
        compiler_params=pltpu.CompilerParams(
            dimension_semantics=("arbitrary", "arbitrary"), vmem_limit_bytes=VMEM_LIMIT),
    )(dn_in, gates, cw, nw, tri)


def _gla_kernel(x_ref, loga_ref, nw_ref, tri_ref, lvl_ref, bmk_ref, bmv_ref, bmvt_ref, o_ref, state_ref,
                *, tb):
    kd_ = GLA_K_DIM
    vd_ = GLA_V_DIM

    @pl.when(pl.program_id(1) == 0)
    def _():
        state_ref[...] = jnp.zeros_like(state_ref)

    rowi = lax.broadcasted_iota(jnp.int32, (CHUNK, kd_), 0)

    def blockdiag_k(kt):
        return jnp.concatenate([kt.astype(BF16)] * GLA_HEADS, axis=0) * bmk_ref[...]

    def chunk(c, carry):
        base = pl.multiple_of(c * CHUNK, CHUNK)
        q = x_ref[pl.ds(base, CHUNK), 0:kd_].astype(F32) * (GLA_HEAD_K ** -0.5)
        k = x_ref[pl.ds(base, CHUNK), kd_:2 * kd_].astype(F32)
        v = x_ref[pl.ds(base, CHUNK), 2 * kd_:2 * kd_ + vd_]
        r = x_ref[pl.ds(base, CHUNK), 2 * kd_ + vd_:2 * kd_ + 2 * vd_].astype(F32)
        la = loga_ref[pl.ds(base, CHUNK), :]
        gc = jnp.dot(tri_ref[...], la, precision=lax.Precision.HIGHEST,
                     preferred_element_type=F32)

        att = lvl_ref[len(GLA_LEVELS)] * _mm_nt(q.astype(BF16), blockdiag_k(k))
        for li, s in enumerate(GLA_LEVELS):
            pos = rowi & (2 * s - 1)
            lower = pos >= s
            if 2 * s >= 8:
                n = CHUNK // (2 * s)
                bnd = gc.reshape(n, 2 * s, kd_)[:, s - 1:s, :]
                gb = jnp.broadcast_to(bnd, (n, 2 * s, kd_)).reshape(CHUNK, kd_)
                dq = gc - gb
                dk = gb - gc
            elif s == 2:
                r1 = pltpu.roll(gc, 1, 0)
                r2 = pltpu.roll(gc, 2, 0)
                up1 = pltpu.roll(gc, CHUNK - 1, 0)
                dq = gc - jnp.where(pos == 2, r1, r2)
                dk = jnp.where(pos == 0, up1, gc) - gc
            else:
                dq = gc - pltpu.roll(gc, 1, 0)
                dk = jnp.zeros_like(gc)
            qt = q * jnp.exp(jnp.where(lower, dq, -jnp.inf))
            kt = k * jnp.exp(jnp.where(lower, -jnp.inf, dk))
            att = att + lvl_ref[li] * _mm_nt(qt.astype(BF16), blockdiag_k(kt))

        glast = gc[CHUNK - 1:CHUNK, :]
        qg = q * jnp.exp(gc)
        kdec = k * jnp.exp(glast - gc)
        st = state_ref[...]
        vblk = jnp.concatenate([v] * GLA_HEADS, axis=0) * bmv_ref[...]
        o = _mm_nt(qg.astype(BF16), st.astype(BF16)) + _mm(att.astype(BF16), vblk)
        upd = _mm_tn(v, kdec.astype(BF16))
        state_ref[...] = st * jnp.exp(glast) + upd * bmvt_ref[...]
        outs = []
        for h in range(GLA_HEADS):
            oh = o[:, h * GLA_HEAD_V:(h + 1) * GLA_HEAD_V]
            outs.append(_rms(oh, nw_ref[...]))
        o_ref[pl.ds(base, CHUNK), :] = (jnp.concatenate(outs, axis=1) * _silu(r)).astype(BF16)
        return carry

    lax.fori_loop(0, tb // CHUNK, chunk, 0)


def _gla_call(gla_in, loga, nw, tri, lvl, bmk, bmv, bmvt, layer, batch):
    t = gla_in.shape[0]
    tb = MIX_TILE
    nb = t // batch // tb
    win = 2 * GLA_K_DIM + 2 * GLA_V_DIM
    const2 = lambda shape: pl.BlockSpec(shape, lambda b, j: (0,) * len(shape))
    return pl.pallas_call(
        functools.partial(_gla_kernel, tb=tb),
        grid=(batch, nb),
        in_specs=[
            pl.BlockSpec((tb, win), lambda b, j: (b * nb + j, 0)),
            pl.BlockSpec((tb, GLA_K_DIM), lambda b, j: (b * nb + j, 0)),
            pl.BlockSpec((None, 1, GLA_HEAD_V), lambda b, j: (layer, 0, 0)),
            const2((CHUNK, CHUNK)),
            const2((len(GLA_LEVELS) + 1, CHUNK, GLA_K_DIM)),
            const2((GLA_K_DIM, GLA_K_DIM)),
            const2((GLA_K_DIM, GLA_V_DIM)),
            const2((GLA_V_DIM, GLA_K_DIM)),
        ],
        out_specs=pl.BlockSpec((tb, GLA_V_DIM), lambda b, j: (b * nb + j, 0)),
        out_shape=jax.ShapeDtypeStruct((t, GLA_V_DIM), BF16),
        scratch_shapes=[pltpu.VMEM((GLA_V_DIM, GLA_K_DIM), F32)],
        compiler_params=pltpu.CompilerParams(
            dimension_semantics=("arbitrary", "arbitrary"), vmem_limit_bytes=VMEM_LIMIT),
    )(gla_in, loga, nw, tri, lvl, bmk, bmv, bmvt)


def _gla_constants():
    i = np.arange(CHUNK)[:, None]
    j = np.arange(CHUNK)[None, :]
    masks = []
    for s in GLA_LEVELS:
        same_block = (i // (2 * s)) == (j // (2 * s))
        masks.append(same_block & ((i % (2 * s)) >= s) & ((j % (2 * s)) < s))
    masks.append(i == j)
    lvl = np.stack([np.tile(m, (1, GLA_HEADS)) for m in masks]).astype(np.float32)
    hr = np.arange(GLA_K_DIM)[:, None] // CHUNK
    bmk = (hr == (np.arange(GLA_K_DIM)[None, :] // GLA_HEAD_K)).astype(np.float32)
    bmv = (hr == (np.arange(GLA_V_DIM)[None, :] // GLA_HEAD_V)).astype(np.float32)
    return (jnp.asarray(lvl), jnp.asarray(bmk, dtype=BF16), jnp.asarray(bmv, dtype=BF16),
            jnp.asarray(bmv.T))


def _memkv_kernel(mem_ref, nw_ref, wk_ref, wv_ref, k_ref, v_ref):
    mb = _rms(mem_ref[...], nw_ref[...]).astype(BF16)
    k_ref[...] = _mm(mb, wk_ref[...]).astype(BF16)
    v_ref[...] = _mm(mb, wv_ref[...]).astype(BF16)


def _memkv_call(mem2d, nw, wk, wv, batch):
    tm, d = mem2d.shape
    m = tm // batch
    depth = wk.shape[0]
    return pl.pallas_call(
        _memkv_kernel,
        grid=(depth, batch),
        in_specs=[
            pl.BlockSpec((m, d), lambda l, b: (b, 0)),
            pl.BlockSpec((None, 1, d), lambda l, b: (l, 0, 0)),
            pl.BlockSpec((None, d, d), lambda l, b: (l, 0, 0)),
            pl.BlockSpec((None, d, d), lambda l, b: (l, 0, 0)),
        ],
        out_specs=[
            pl.BlockSpec((None, m, d), lambda l, b: (l, b, 0)),
            pl.BlockSpec((None, m, d), lambda l, b: (l, b, 0)),
        ],
        out_shape=[
            jax.ShapeDtypeStruct((depth, tm, d), BF16),
            jax.ShapeDtypeStruct((depth, tm, d), BF16),
        ],
        compiler_params=pltpu.CompilerParams(
            dimension_semantics=("arbitrary", "arbitrary"), vmem_limit_bytes=VMEM_LIMIT),
    )(mem2d, nw, wk, wv)


def _xa_kernel(h_ref, odn_ref, ogla_ref, wout_ref, nw_ref, wq_ref, k_ref, v_ref, wo_ref, out_ref):
    d = h_ref.shape[1]
    hd = d // XA_HEADS
    h1 = (h_ref[...] + _mm(odn_ref[...], wout_ref[0:DN_DIM, :])
          + _mm(ogla_ref[...], wout_ref[DN_DIM:DN_DIM + GLA_V_DIM, :]))
    xb = _rms(h1, nw_ref[...]).astype(BF16)
    q = _mm(xb, wq_ref[...]).astype(BF16)
    outs = []
    for h in range(XA_HEADS):
        s = _mm_nt(q[:, h * hd:(h + 1) * hd], k_ref[:, h * hd:(h + 1) * hd]) * (hd ** -0.5)
        p = jnp.exp(s - jnp.max(s, axis=-1, keepdims=True))
        p = p / jnp.sum(p, axis=-1, keepdims=True)
        outs.append(_mm(p.astype(BF16), v_ref[:, h * hd:(h + 1) * hd]).astype(BF16))
    out_ref[...] = h1 + _mm(jnp.concatenate(outs, axis=1), wo_ref[...])


def _xa_call(h, odn, ogla, wout, nw, wq, kmem, vmem, wo, layer, batch):
    t, d = h.shape
    tm = ROW_TILE
    nb = t // batch // tm
    m = kmem.shape[1] // batch
    wspec = lambda shape: pl.BlockSpec((None,) + shape, lambda b, j: (layer, 0, 0))
    return pl.pallas_call(
        _xa_kernel,
        grid=(batch, nb),
        in_specs=[
            pl.BlockSpec((tm, d), lambda b, j: (b * nb + j, 0)),
            pl.BlockSpec((tm, DN_DIM), lambda b, j: (b * nb + j, 0)),
            pl.BlockSpec((tm, GLA_V_DIM), lambda b, j: (b * nb + j, 0)),
            wspec((DN_DIM + GLA_V_DIM, d)),
            wspec((1, d)),
            wspec((d, d)),
            pl.BlockSpec((None, m, d), lambda b, j: (layer, b, 0)),
            pl.BlockSpec((None, m, d), lambda b, j: (layer, b, 0)),
            wspec((d, d)),
        ],
        out_specs=pl.BlockSpec((tm, d), lambda b, j: (b * nb + j, 0)),
        out_shape=jax.ShapeDtypeStruct((t, d), F32),
        compiler_params=pltpu.CompilerParams(
            dimension_semantics=("arbitrary", "arbitrary"), vmem_limit_bytes=VMEM_LIMIT),
    )(h, odn, ogla, wout, nw, wq, kmem, vmem, wo)


def _mlp_kernel(h_ref, nw_ref, w1_ref, w2_ref, fw_ref, out_ref, *, final):
    h = h_ref[...]
    xb = _rms(h, nw_ref[...]).astype(BF16)
    acc = h
    for c in range(w1_ref.shape[1] // FF_TILE):
        a = jnp.maximum(_mm(xb, w1_ref[:, c * FF_TILE:(c + 1) * FF_TILE]), 0.0)
        acc = acc + _mm((a * a).astype(BF16), w2_ref[c * FF_TILE:(c + 1) * FF_TILE, :])
    out_ref[...] = _rms(acc, fw_ref[...]) if final else acc


def _mlp_call(h, nw, w1, w2, fw, layer, final):
    t, d = h.shape
    ff = w1.shape[2]
    tm = ROW_TILE
    wspec = lambda shape: pl.BlockSpec((None,) + shape, lambda i: (layer, 0, 0))
    return pl.pallas_call(
        functools.partial(_mlp_kernel, final=final),
        grid=(t // tm,),
        in_specs=[
            pl.BlockSpec((tm, d), lambda i: (i, 0)),
            wspec((1, d)),
            wspec((d, ff)),
            wspec((ff, d)),
            pl.BlockSpec((1, d), lambda i: (0, 0)),
        ],
        out_specs=pl.BlockSpec((tm, d), lambda i: (i, 0)),
        out_shape=jax.ShapeDtypeStruct((t, d), F32),
        compiler_params=pltpu.CompilerParams(
            dimension_semantics=("arbitrary",), vmem_limit_bytes=VMEM_LIMIT),
    )(h, nw, w1, w2, fw)


def kernel(x, mem, mix_norm_w, w_in, conv_w, dn_a_log, dn_dt_bias, dn_norm_w, gla_w_gate2, gla_b_gate,
           gla_norm_w, w_out, xa_norm_w, mem_norm_w, xa_wq, xa_wk, xa_wv, xa_wo, mlp_norm_w, mlp_w1,
           mlp_w2, final_norm_w):
    batch, seq, d = x.shape
    depth = w_in.shape[0]
    t = batch * seq
    assert seq % MIX_TILE == 0 and seq % ROW_TILE == 0 and MIX_TILE % CHUNK == 0

    sizes = (DN_DIM, DN_DIM, DN_DIM, DN_DIM, DN_HEADS, DN_HEADS,
             GLA_K_DIM, GLA_K_DIM, GLA_V_DIM, GLA_V_DIM, GLA_LOWRANK)
    offs = np.concatenate([[0], np.cumsum(sizes)])
    o_db, o_gq, o_glr = int(offs[4]), int(offs[6]), int(offs[10])
    n_small = 2 * DN_HEADS + GLA_LOWRANK
    w_dn = w_in[:, :, 0:o_db].astype(BF16)
    w_gla = w_in[:, :, o_gq:o_glr].astype(BF16)
    w_sm = jnp.concatenate(
        [w_in[:, :, o_db:o_gq], w_in[:, :, o_glr:],
         jnp.zeros((depth, d, SMALL_W - n_small), F32)], axis=2).astype(BF16)
    wg2 = jnp.zeros((depth, SMALL_W, GLA_K_DIM), F32)
    wg2 = wg2.at[:, 2 * DN_HEADS:n_small, :].set(gla_w_gate2).astype(BF16)
    gpar = jnp.zeros((depth, 8, SMALL_W), F32)
    gpar = gpar.at[:, 0, DN_HEADS:2 * DN_HEADS].set(dn_a_log)
    gpar = gpar.at[:, 1, DN_HEADS:2 * DN_HEADS].set(dn_dt_bias)
    bg = gla_b_gate.reshape(depth, 1, GLA_K_DIM)
    cw = jnp.zeros((depth, 8, 3 * DN_DIM), F32).at[:, 0:CONV_WIDTH, :].set(conv_w)
    dn_nw = dn_norm_w.reshape(depth, 1, DN_HEAD_DIM)
    gla_nw = gla_norm_w.reshape(depth, 1, GLA_HEAD_V)
    tri = jnp.asarray(np.tril(np.ones((CHUNK, CHUNK), np.float32)))
    lvl, bmk, bmv, bmvt = _gla_constants()

    mix_nw = mix_norm_w.reshape(depth, 1, d)
    xa_nw = xa_norm_w.reshape(depth, 1, d)
    mem_nw = mem_norm_w.reshape(depth, 1, d)
    mlp_nw = mlp_norm_w.reshape(depth, 1, d)
    wout_b = w_out.astype(BF16)
    wq_b = xa_wq.astype(BF16)
    wo_b = xa_wo.astype(BF16)
    w1_b = mlp_w1.astype(BF16)
    w2_b = mlp_w2.astype(BF16)
    fw = final_norm_w.reshape(1, d)

    kmem, vmem = _memkv_call(mem.reshape(batch * mem.shape[1], d), mem_nw,
                             xa_wk.astype(BF16), xa_wv.astype(BF16), batch)

    h = x.reshape(t, d)
    for l in range(depth):
        dn_in, gla_in, gates, loga = _proj_call(h, mix_nw, w_dn, w_gla, w_sm, wg2, gpar, bg, l)
        o_dn = _dn_call(dn_in, gates, cw, dn_nw, tri, l, batch)
        o_gla = _gla_call(gla_in, loga, gla_nw, tri, lvl, bmk, bmv, bmvt, l, batch)
        h = _xa_call(h, o_dn, o_gla, wout_b, xa_nw, wq_b, kmem, vmem, wo_b, l, batch)
        h = _mlp_call(h, mlp_nw, w1_b, w2_b, fw, l, l == depth - 1)
    return h.reshape(batch, seq, d)
```

```python
import functools

import jax
import jax.numpy as jnp
import numpy as np
from jax import lax
from jax.experimental import pallas as pl
from jax.experimental.pallas import tpu as pltpu

F32 = jnp.float32
BF16 = jnp.bfloat16

EPS = 1e-6
CHUNK = 64
CONV_WIDTH = 4
DN_HEADS = 4
DN_HEAD_DIM = 128
DN_DIM = DN_HEADS * DN_HEAD_DIM
GLA_HEADS = 4
GLA_HEAD_K = 64
GLA_HEAD_V = 128
GLA_K_DIM = GLA_HEADS * GLA_HEAD_K
GLA_V_DIM = GLA_HEADS * GLA_HEAD_V
GLA_LOWRANK = 16
GLA_TAU = 16.0
XA_HEADS = 4
LANES = 128
SMALL_W = LANES
VMEM_LIMIT = 56 * 1024 * 1024

ROW_TILE = 512
MIX_TILE = 512
FF_TILE = 1024
DN_GROUP = 4

GLA_LEVELS = (32, 16, 8, 4, 2, 1)


def _mm(a, b):
    return jnp.dot(a, b, preferred_element_type=F32)


def _mm_nt(a, b):
    return lax.dot_general(a, b, (((1,), (1,)), ((), ())), preferred_element_type=F32)


def _mm_tn(a, b):
    return lax.dot_general(a, b, (((0,), (0,)), ((), ())), preferred_element_type=F32)


def _rms(x, w):
    return x * lax.rsqrt(jnp.mean(x * x, axis=-1, keepdims=True) + EPS) * w


def _softplus(x):
    return jnp.maximum(x, 0.0) + jnp.log1p(jnp.exp(-jnp.abs(x)))


def _silu(x):
    return x * jax.nn.sigmoid(x)


def _proj_kernel(h_ref, nw_ref, wdn_ref, wgla_ref, wsm_ref, wg2_ref, gpar_ref, bg_ref,
                 dn_ref, gla_ref, gates_ref, loga_ref):
    xb = _rms(h_ref[...], nw_ref[...]).astype(BF16)
    dn_ref[...] = _mm(xb, wdn_ref[...]).astype(BF16)
    gla_ref[...] = _mm(xb, wgla_ref[...]).astype(BF16)
    sm = _mm(xb, wsm_ref[...])
    lane = lax.broadcasted_iota(jnp.int32, sm.shape, 1)
    beta = jax.nn.sigmoid(sm)
    g = -jnp.exp(gpar_ref[0:1, :]) * _softplus(sm + gpar_ref[1:2, :])
    gates_ref[...] = jnp.where(lane < DN_HEADS, beta, g)
    logit = _mm(sm.astype(BF16), wg2_ref[...]) + bg_ref[...]
    loga_ref[...] = -_softplus(-logit) * (1.0 / GLA_TAU)


def _proj_call(h, nw, wdn, wgla, wsm, wg2, gpar, bg, layer):
    t, d = h.shape
    tm = ROW_TILE
    wspec = lambda shape: pl.BlockSpec((None,) + shape, lambda i: (layer, 0, 0))
    return pl.pallas_call(
        _proj_kernel,
        grid=(t // tm,),
        in_specs=[
            pl.BlockSpec((tm, d), lambda i: (i, 0)),
            wspec((1, d)),
            wspec((d, 4 * DN_DIM)),
            wspec((d, 2 * GLA_K_DIM + 2 * GLA_V_DIM)),
            wspec((d, SMALL_W)),
            wspec((SMALL_W, GLA_K_DIM)),
            wspec((8, SMALL_W)),
            wspec((1, GLA_K_DIM)),
        ],
        out_specs=[
            pl.BlockSpec((tm, 4 * DN_DIM), lambda i: (i, 0)),
            pl.BlockSpec((tm, 2 * GLA_K_DIM + 2 * GLA_V_DIM), lambda i: (i, 0)),
            pl.BlockSpec((tm, SMALL_W), lambda i: (i, 0)),
            pl.BlockSpec((tm, GLA_K_DIM), lambda i: (i, 0)),
        ],
        out_shape=[
            jax.ShapeDtypeStruct((t, 4 * DN_DIM), BF16),
            jax.ShapeDtypeStruct((t, 2 * GLA_K_DIM + 2 * GLA_V_DIM), BF16),
            jax.ShapeDtypeStruct((t, SMALL_W), F32),
            jax.ShapeDtypeStruct((t, GLA_K_DIM), F32),
        ],
        compiler_params=pltpu.CompilerParams(
            dimension_semantics=("arbitrary",), vmem_limit_bytes=VMEM_LIMIT),
    )(h, nw, wdn, wgla, wsm, wg2, gpar, bg)


def _dn_kernel(x_ref, gates_ref, cw_ref, nw_ref, tri_ref, o_ref,
               xpad_ref, state_ref, qgw_ref, u_ref, lhs2_ref, gl_ref, obuf_ref, *, tb):
    c3 = 3 * DN_DIM
    hd = DN_HEAD_DIM

    @pl.when(pl.program_id(1) == 0)
    def _():
        state_ref[...] = jnp.zeros_like(state_ref)
        xpad_ref[0:8, :] = jnp.zeros((8, c3), F32)

    xpad_ref[8:8 + tb, :] = x_ref[:, 0:c3].astype(F32)

    row = lax.broadcasted_iota(jnp.int32, (CHUNK, CHUNK), 0)
    col = lax.broadcasted_iota(jnp.int32, (CHUNK, CHUNK), 1)
    causal = row >= col
    strict = row > col

    g = DN_GROUP
    gc_rows = g * CHUNK
    heads = range(DN_HEADS)
    bmm = functools.partial(jnp.einsum, "cij,cjl->cil", preferred_element_type=F32)
    bmm_nt = functools.partial(jnp.einsum, "cik,cjk->cij", preferred_element_type=F32)

    def prepare(gi, carry):
        base = pl.multiple_of(gi * gc_rows, gc_rows)
        cbase = gi * g

        def conv(col):
            win = xpad_ref[pl.ds(base, gc_rows + 8), col:col + hd]
            y = win[8:] * cw_ref[3:4, col:col + hd]
            for w in range(CONV_WIDTH - 1):
                shifted = pltpu.roll(win, CONV_WIDTH - 1 - w, 0)
                y = y + shifted[8:] * cw_ref[w:w + 1, col:col + hd]
            return _silu(y)

        gt = gates_ref[pl.ds(base, gc_rows), :].reshape(g, CHUNK, SMALL_W)
        tri_b = jnp.broadcast_to(tri_ref[...], (g, CHUNK, CHUNK))
        gc = jnp.einsum("cij,cjl->cil", tri_b, gt, precision=lax.Precision.HIGHEST,
                        preferred_element_type=F32)
        gct = jnp.swapaxes(gc, 1, 2)
        q, k, v, gcol, bcol, glast, decay, a, aqk = ([None] * DN_HEADS for _ in range(9))
        for h in heads:
            qh = conv(h * hd)
            kh = conv(DN_DIM + h * hd)
            qh = qh * lax.rsqrt(jnp.sum(qh * qh, axis=-1, keepdims=True) + EPS) * (hd ** -0.5)
            kh = kh * lax.rsqrt(jnp.sum(kh * kh, axis=-1, keepdims=True) + EPS)
            q[h] = qh.reshape(g, CHUNK, hd)
            k[h] = kh.reshape(g, CHUNK, hd)
            v[h] = conv(2 * DN_DIM + h * hd).reshape(g, CHUNK, hd)
            gcol[h] = gc[:, :, DN_HEADS + h:DN_HEADS + h + 1]
            grow = gct[:, DN_HEADS + h:DN_HEADS + h + 1, :]
            bcol[h] = gt[:, :, h:h + 1]
            glast[h] = gc[:, CHUNK - 1:CHUNK, DN_HEADS + h:DN_HEADS + h + 1]
            decay[h] = jnp.exp(jnp.where(causal, gcol[h] - grow, -jnp.inf))
        for h in heads:
            kb = k[h].astype(BF16)
            a[h] = jnp.where(strict, bcol[h] * bmm_nt(kb, kb) * decay[h], 0.0)
            aqk[h] = bmm_nt(q[h].astype(BF16), kb) * decay[h]
        tm = [-a[h] for h in heads]
        xf = [None] * DN_HEADS
        for h in heads:
            ab = a[h].astype(BF16)
            xf[h] = bmm(ab, ab)
        for it in range(5):
            xb = [xf[h].astype(BF16) for h in heads]
            for h in heads:
                tm[h] = tm[h] + xf[h] + bmm(tm[h].astype(BF16), xb[h])
            if it < 4:
                for h in heads:
                    xf[h] = bmm(xb[h], xb[h])
        for h in heads:
            egc = jnp.exp(gcol[h])
            rhs = jnp.concatenate([k[h] * (bcol[h] * egc), v[h] * bcol[h]], axis=2)
            wu = rhs + bmm(tm[h].astype(BF16), rhs.astype(BF16))
            qgw_ref[h, pl.ds(cbase, g)] = jnp.concatenate(
                [q[h] * egc, wu[:, :, 0:hd]], axis=1).astype(BF16)
            u_ref[h, pl.ds(cbase, g)] = wu[:, :, hd:2 * hd]
            kdt = jnp.swapaxes(k[h] * jnp.exp(glast[h] - gcol[h]), 1, 2)
            lhs2_ref[h, pl.ds(cbase, g)] = jnp.concatenate([aqk[h], kdt], axis=1).astype(BF16)
            gl_ref[h, pl.ds(cbase, g)] = jnp.broadcast_to(jnp.exp(glast[h]), (g, 8, hd))
        return carry

    def recur(c, carry):
        s_old = [state_ref[h] for h in heads]
        r1 = [_mm(qgw_ref[h, c], s_old[h].astype(BF16)) for h in heads]
        vb = [(u_ref[h, c] - r1[h][CHUNK:2 * CHUNK]).astype(BF16) for h in heads]
        r2 = [_mm(lhs2_ref[h, c], vb[h]) for h in heads]
        for h in heads:
            decayed = (s_old[h].reshape(hd // 8, 8, hd) * gl_ref[h, c][None]).reshape(hd, hd)
            state_ref[h] = decayed + r2[h][CHUNK:CHUNK + hd]
            obuf_ref[pl.ds(pl.multiple_of(c * CHUNK, CHUNK), CHUNK), h * hd:(h + 1) * hd] = (
                r1[h][0:CHUNK] + r2[h][0:CHUNK])
        return carry

    lax.fori_loop(0, tb // gc_rows, prepare, 0)
    lax.fori_loop(0, tb // CHUNK, recur, 0)
    for h in heads:
        z = x_ref[:, c3 + h * hd:c3 + (h + 1) * hd].astype(F32)
        o_ref[:, h * hd:(h + 1) * hd] = (
            _rms(obuf_ref[:, h * hd:(h + 1) * hd], nw_ref[...]) * _silu(z)).astype(BF16)
    xpad_ref[0:8, :] = xpad_ref[tb:tb + 8, :]


def _dn_call(dn_in, gates, cw, nw, tri, layer, batch):
    t = dn_in.shape[0]
    tb = MIX_TILE
    nb = t // batch // tb
    nc = tb // CHUNK
    return pl.pallas_call(
        functools.partial(_dn_kernel, tb=tb),
        grid=(batch, nb),
        in_specs=[
            pl.BlockSpec((tb, 4 * DN_DIM), lambda b, j: (b * nb + j, 0)),
            pl.BlockSpec((tb, SMALL_W), lambda b, j: (b * nb + j, 0)),
            pl.BlockSpec((None, 8, 3 * DN_DIM), lambda b, j: (layer, 0, 0)),
            pl.BlockSpec((None, 1, DN_HEAD_DIM), lambda b, j: (layer, 0, 0)),
            pl.BlockSpec((CHUNK, CHUNK), lambda b, j: (0, 0)),
        ],
        out_specs=pl.BlockSpec((tb, DN_DIM), lambda b, j: (b * nb + j, 0)),
        out_shape=jax.ShapeDtypeStruct((t, DN_DIM), BF16),
        scratch_shapes=[
            pltpu.VMEM((tb + 8, 3 * DN_DIM), F32),
            pltpu.VMEM((DN_HEADS, DN_HEAD_DIM, DN_HEAD_DIM), F32),
            pltpu.VMEM((DN_HEADS, nc, 2 * CHUNK, DN_HEAD_DIM), BF16),
            pltpu.VMEM((DN_HEADS, nc, CHUNK, DN_HEAD_DIM), F32),
            pltpu.VMEM((DN_HEADS, nc, CHUNK + DN_HEAD_DIM, CHUNK), BF16),
            pltpu.VMEM((DN_HEADS, nc, 8, DN_HEAD_DIM), F32),
            pltpu.VMEM((tb, DN_DIM), F32),
        ],
        compiler_params=pltpu.CompilerParams(
            dimension_semantics=("arbitrary", "arbitrary"), vmem_limit_bytes=VMEM_LIMIT),
    )(dn_in, gates, cw, nw, tri)


def _gla_kernel(x_ref, loga_ref, nw_ref, tri_ref, lvl_ref, bmk_ref, bmv_ref, bmvt_ref, o_ref, state_ref,
                *, tb):
    kd_ = GLA_K_DIM
    vd_ = GLA_V_DIM

    @pl.when(pl.program_id(1) == 0)
    def _():
        state_ref[...] = jnp.zeros_like(state_ref)

    rowi = lax.broadcasted_iota(jnp.int32, (CHUNK, kd_), 0)

    def blockdiag_k(kt):
        return jnp.concatenate([kt.astype(BF16)] * GLA_HEADS, axis=0) * bmk_ref[...]

    def chunk(c, carry):
        base = pl.multiple_of(c * CHUNK, CHUNK)
        q = x_ref[pl.ds(base, CHUNK), 0:kd_].astype(F32) * (GLA_HEAD_K ** -0.5)
        k = x_ref[pl.ds(base, CHUNK), kd_:2 * kd_].astype(F32)
        v = x_ref[pl.ds(base, CHUNK), 2 * kd_:2 * kd_ + vd_]
        r = x_ref[pl.ds(base, CHUNK), 2 * kd_ + vd_:2 * kd_ + 2 * vd_].astype(F32)
        la = loga_ref[pl.ds(base, CHUNK), :]
        gc = jnp.dot(tri_ref[...], la, precision=lax.Precision.HIGHEST,
                     preferred_element_type=F32)

        att = lvl_ref[len(GLA_LEVELS)] * _mm_nt(q.astype(BF16), blockdiag_k(k))
        for li, s in enumerate(GLA_LEVELS):
            pos = rowi & (2 * s - 1)
            lower = pos >= s
            if 2 * s >= 8:
                n = CHUNK // (2 * s)
                bnd = gc.reshape(n, 2 * s, kd_)[:, s - 1:s, :]
                gb = jnp.broadcast_to(bnd, (n, 2 * s, kd_)).reshape(CHUNK, kd_)
                dq = gc - gb
                dk = gb - gc
            elif s == 2:
                r1 = pltpu.roll(gc, 1, 0)
                r2 = pltpu.roll(gc, 2, 0)
                up1 = pltpu.roll(gc, CHUNK - 1, 0)
                dq = gc - jnp.where(pos == 2, r1, r2)
                dk = jnp.where(pos == 0, up1, gc) - gc
            else:
                dq = gc - pltpu.roll(gc, 1, 0)
                dk = jnp.zeros_like(gc)
            qt = q * jnp.exp(jnp.where(lower, dq, -jnp.inf))
            kt = k * jnp.exp(jnp.where(lower, -jnp.inf, dk))
            att = att + lvl_ref[li] * _mm_nt(qt.astype(BF16), blockdiag_k(kt))

        glast = gc[CHUNK - 1:CHUNK, :]
        qg = q * jnp.exp(gc)
        kdec = k * jnp.exp(glast - gc)
        st = state_ref[...]
        vblk = jnp.concatenate([v] * GLA_HEADS, axis=0) * bmv_ref[...]
        o = _mm_nt(qg.astype(BF16), st.astype(BF16)) + _mm(att.astype(BF16), vblk)
        upd = _mm_tn(v, kdec.astype(BF16))
        state_ref[...] = st * jnp.exp(glast) + upd * bmvt_ref[...]
        outs = []
        for h in range(GLA_HEADS):
            oh = o[:, h * GLA_HEAD_V:(h + 1) * GLA_HEAD_V]
            outs.append(_rms(oh, nw_ref[...]))
        o_ref[pl.ds(base, CHUNK), :] = (jnp.concatenate(outs, axis=1) * _silu(r)).astype(BF16)
        return carry

    lax.fori_loop(0, tb // CHUNK, chunk, 0)


def _gla_call(gla_in, loga, nw, tri, lvl, bmk, bmv, bmvt, layer, batch):
    t = gla_in.shape[0]
    tb = MIX_TILE
    nb = t // batch // tb
    win = 2 * GLA_K_DIM + 2 * GLA_V_DIM
    const2 = lambda shape: pl.BlockSpec(shape, lambda b, j: (0,) * len(shape))
    return pl.pallas_call(
        functools.partial(_gla_kernel, tb=tb),
        grid=(batch, nb),
        in_specs=[
            pl.BlockSpec((tb, win), lambda b, j: (b * nb + j, 0)),
            pl.BlockSpec((tb, GLA_K_DIM), lambda b, j: (b * nb + j, 0)),
            pl.BlockSpec((None, 1, GLA_HEAD_V), lambda b, j: (layer, 0, 0)),
            const2((CHUNK, CHUNK)),
            const2((len(GLA_LEVELS) + 1, CHUNK, GLA_K_DIM)),
            const2((GLA_K_DIM, GLA_K_DIM)),
            const2((GLA_K_DIM, GLA_V_DIM)),
            const2((GLA_V_DIM, GLA_K_DIM)),
        ],
        out_specs=pl.BlockSpec((tb, GLA_V_DIM), lambda b, j: (b * nb + j, 0)),
        out_shape=jax.ShapeDtypeStruct((t, GLA_V_DIM), BF16),
        scratch_shapes=[pltpu.VMEM((GLA_V_DIM, GLA_K_DIM), F32)],
        compiler_params=pltpu.CompilerParams(
            dimension_semantics=("arbitrary", "arbitrary"), vmem_limit_bytes=VMEM_LIMIT),
    )(gla_in, loga, nw, tri, lvl, bmk, bmv, bmvt)


def _gla_constants():
    i = np.arange(CHUNK)[:, None]
    j = np.arange(CHUNK)[None, :]
    masks = []
    for s in GLA_LEVELS:
        same_block = (i // (2 * s)) == (j // (2 * s))
        masks.append(same_block & ((i % (2 * s)) >= s) & ((j % (2 * s)) < s))
    masks.append(i == j)
    lvl = np.stack([np.tile(m, (1, GLA_HEADS)) for m in masks]).astype(np.float32)
    hr = np.arange(GLA_K_DIM)[:, None] // CHUNK
    bmk = (hr == (np.arange(GLA_K_DIM)[None, :] // GLA_HEAD_K)).astype(np.float32)
    bmv = (hr == (np.arange(GLA_V_DIM)[None, :] // GLA_HEAD_V)).astype(np.float32)
    return (jnp.asarray(lvl), jnp.asarray(bmk, dtype=BF16), jnp.asarray(bmv, dtype=BF16),
            jnp.asarray(bmv.T))


def _memkv_kernel(mem_ref, nw_ref, wk_ref, wv_ref, k_ref, v_ref):
    mb = _rms(mem_ref[...], nw_ref[...]).astype(BF16)
    k_ref[...] = _mm(mb, wk_ref[...]).astype(BF16)
    v_ref[...] = _mm(mb, wv_ref[...]).astype(BF16)


def _memkv_call(mem2d, nw, wk, wv, batch):
    tm, d = mem2d.shape
    m = tm // batch
    depth = wk.shape[0]
    return pl.pallas_call(
        _memkv_kernel,
        grid=(depth, batch),
        in_specs=[
            pl.BlockSpec((m, d), lambda l, b: (b, 0)),
            pl.BlockSpec((None, 1, d), lambda l, b: (l, 0, 0)),
            pl.BlockSpec((None, d, d), lambda l, b: (l, 0, 0)),
            pl.BlockSpec((None, d, d), lambda l, b: (l, 0, 0)),
        ],
        out_specs=[
            pl.BlockSpec((None, m, d), lambda l, b: (l, b, 0)),
            pl.BlockSpec((None, m, d), lambda l, b: (l, b, 0)),
        ],
        out_shape=[
            jax.ShapeDtypeStruct((depth, tm, d), BF16),
            jax.ShapeDtypeStruct((depth, tm, d), BF16),
        ],
        compiler_params=pltpu.CompilerParams(
            dimension_semantics=("arbitrary", "arbitrary"), vmem_limit_bytes=VMEM_LIMIT),
    )(mem2d, nw, wk, wv)


def _xa_kernel(h_ref, odn_ref, ogla_ref, wout_ref, nw_ref, wq_ref, k_ref, v_ref, wo_ref, out_ref):
    d = h_ref.shape[1]
    hd = d // XA_HEADS
    h1 = (h_ref[...] + _mm(odn_ref[...], wout_ref[0:DN_DIM, :])
          + _mm(ogla_ref[...], wout_ref[DN_DIM:DN_DIM + GLA_V_DIM, :]))
    xb = _rms(h1, nw_ref[...]).astype(BF16)
    q = _mm(xb, wq_ref[...]).astype(BF16)
    outs = []
    for h in range(XA_HEADS):
        s = _mm_nt(q[:, h * hd:(h + 1) * hd], k_ref[:, h * hd:(h + 1) * hd]) * (hd ** -0.5)
        p = jnp.exp(s - jnp.max(s, axis=-1, keepdims=True))
        p = p / jnp.sum(p, axis=-1, keepdims=True)
        outs.append(_mm(p.astype(BF16), v_ref[:, h * hd:(h + 1) * hd]).astype(BF16))
    out_ref[...] = h1 + _mm(jnp.concatenate(outs, axis=1), wo_ref[...])


def _xa_call(h, odn, ogla, wout, nw, wq, kmem, vmem, wo, layer, batch):
    t, d = h.shape
    tm = ROW_TILE
    nb = t // batch // tm
    m = kmem.shape[1] // batch
    wspec = lambda shape: pl.BlockSpec((None,) + shape, lambda b, j: (layer, 0, 0))
    return pl.pallas_call(
        _xa_kernel,
        grid=(batch, nb),
        in_specs=[
            pl.BlockSpec((tm, d), lambda b, j: (b * nb + j, 0)),
            pl.BlockSpec((tm, DN_DIM), lambda b, j: (b * nb + j, 0)),
            pl.BlockSpec((tm, GLA_V_DIM), lambda b, j: (b * nb + j, 0)),
            wspec((DN_DIM + GLA_V_DIM, d)),
            wspec((1, d)),
            wspec((d, d)),
            pl.BlockSpec((None, m, d), lambda b, j: (layer, b, 0)),
            pl.BlockSpec((None, m, d), lambda b, j: (layer, b, 0)),
            wspec((d, d)),
        ],
        out_specs=pl.BlockSpec((tm, d), lambda b, j: (b * nb + j, 0)),
        out_shape=jax.ShapeDtypeStruct((t, d), F32),
        compiler_params=pltpu.CompilerParams(
            dimension_semantics=("arbitrary", "arbitrary"), vmem_limit_bytes=VMEM_LIMIT),
    )(h, odn, ogla, wout, nw, wq, kmem, vmem, wo)


def _mlp_kernel(h_ref, nw_ref, w1_ref, w2_ref, fw_ref, out_ref, *, final):
    h = h_ref[...]
    xb = _rms(h, nw_ref[...]).astype(BF16)
    acc = h
    for c in range(w1_ref.shape[1] // FF_TILE):
        a = jnp.maximum(_mm(xb, w1_ref[:, c * FF_TILE:(c + 1) * FF_TILE]), 0.0)
        acc = acc + _mm((a * a).astype(BF16), w2_ref[c * FF_TILE:(c + 1) * FF_TILE, :])
    out_ref[...] = _rms(acc, fw_ref[...]) if final else acc


def _mlp_call(h, nw, w1, w2, fw, layer, final):
    t, d = h.shape
    ff = w1.shape[2]
    tm = ROW_TILE
    wspec = lambda shape: pl.BlockSpec((None,) + shape, lambda i: (layer, 0, 0))
    return pl.pallas_call(
        functools.partial(_mlp_kernel, final=final),
        grid=(t // tm,),
        in_specs=[
            pl.BlockSpec((tm, d), lambda i: (i, 0)),
            wspec((1, d)),
            wspec((d, ff)),
            wspec((ff, d)),
            pl.BlockSpec((1, d), lambda i: (0, 0)),
        ],
        out_specs=pl.BlockSpec((tm, d), lambda i: (i, 0)),
        out_shape=jax.ShapeDtypeStruct((t, d), F32),
        compiler_params=pltpu.CompilerParams(
            dimension_semantics=("arbitrary",), vmem_limit_bytes=VMEM_LIMIT),
    )(h, nw, w1, w2, fw)


def kernel(x, mem, mix_norm_w, w_in, conv_w, dn_a_log, dn_dt_bias, dn_norm_w, gla_w_gate2, gla_b_gate,
           gla_norm_w, w_out, xa_norm_w, mem_norm_w, xa_wq, xa_wk, xa_wv, xa_wo, mlp_norm_w, mlp_w1,
           mlp_w2, final_norm_w):
    batch, seq, d = x.shape
    depth = w_in.shape[0]
    t = batch * seq
    assert seq % MIX_TILE == 0 and seq % ROW_TILE == 0 and MIX_TILE % CHUNK == 0

    sizes = (DN_DIM, DN_DIM, DN_DIM, DN_DIM, DN_HEADS, DN_HEADS,
             GLA_K_DIM, GLA_K_DIM, GLA_V_DIM, GLA_V_DIM, GLA_LOWRANK)
    offs = np.concatenate([[0], np.cumsum(sizes)])
    o_db, o_gq, o_glr = int(offs[4]), int(offs[6]), int(offs[10])
    n_small = 2 * DN_HEADS + GLA_LOWRANK
    w_dn = w_in[:, :, 0:o_db].astype(BF16)
    w_gla = w_in[:, :, o_gq:o_glr].astype(BF16)
    w_sm = jnp.concatenate(
        [w_in[:, :, o_db:o_gq], w_in[:, :, o_glr:],
         jnp.zeros((depth, d, SMALL_W - n_small), F32)], axis=2).astype(BF16)
    wg2 = jnp.zeros((depth, SMALL_W, GLA_K_DIM), F32)
    wg2 = wg2.at[:, 2 * DN_HEADS:n_small, :].set(gla_w_gate2).astype(BF16)
    gpar = jnp.zeros((depth, 8, SMALL_W), F32)
    gpar = gpar.at[:, 0, DN_HEADS:2 * DN_HEADS].set(dn_a_log)
    gpar = gpar.at[:, 1, DN_HEADS:2 * DN_HEADS].set(dn_dt_bias)
    bg = gla_b_gate.reshape(depth, 1, GLA_K_DIM)
    cw = jnp.zeros((depth, 8, 3 * DN_DIM), F32).at[:, 0:CONV_WIDTH, :].set(conv_w)
    dn_nw = dn_norm_w.reshape(depth, 1, DN_HEAD_DIM)
    gla_nw = gla_norm_w.reshape(depth, 1, GLA_HEAD_V)
    tri = jnp.asarray(np.tril(np.ones((CHUNK, CHUNK), np.float32)))
    lvl, bmk, bmv, bmvt = _gla_constants()

    mix_nw = mix_norm_w.reshape(depth, 1, d)
    xa_nw = xa_norm_w.reshape(depth, 1, d)
    mem_nw = mem_norm_w.reshape(depth, 1, d)
    mlp_nw = mlp_norm_w.reshape(depth, 1, d)
    wout_b = w_out.astype(BF16)
    wq_b = xa_wq.astype(BF16)
    wo_b = xa_wo.astype(BF16)
    w1_b = mlp_w1.astype(BF16)
    w2_b = mlp_w2.astype(BF16)
    fw = final_norm_w.reshape(1, d)

    kmem, vmem = _memkv_call(mem.reshape(batch * mem.shape[1], d), mem_nw,
                             xa_wk.astype(BF16), xa_wv.astype(BF16), batch)

    h = x.reshape(t, d)
    for l in range(depth):
        dn_in, gla_in, gates, loga = _proj_call(h, mix_nw, w_dn, w_gla, w_sm, wg2, gpar, bg, l)
        o_dn = _dn_call(dn_in, gates, cw, dn_nw, tri, l, batch)
        o_gla = _gla_call(gla_in, loga, gla_nw, tri, lvl, bmk, bmv, bmvt, l, batch)
        h = _xa_call(h, o_dn, o_gla, wout_b, xa_nw, wq_b, kmem, vmem, wo_b, l, batch)
        h = _mlp_call(h, mlp_nw, w1_b, w2_b, fw, l, l == depth - 1)
    return h.reshape(batch, seq, d)
```

```python
import functools

import jax
import jax.numpy as jnp
import numpy as np
from jax import lax
from jax.experimental import pallas as pl
from jax.experimental.pallas import tpu as pltpu

F32 = jnp.float32
BF16 = jnp.bfloat16

EPS = 1e-6
CHUNK = 64
CONV_WIDTH = 4
DN_HEADS = 4
DN_HEAD_DIM = 128
DN_DIM = DN_HEADS * DN_HEAD_DIM
GLA_HEADS = 4
GLA_HEAD_K = 64
GLA_HEAD_V = 128
GLA_K_DIM = GLA_HEADS * GLA_HEAD_K
GLA_V_DIM = GLA_HEADS * GLA_HEAD_V
GLA_LOWRANK = 16
GLA_TAU = 16.0
XA_HEADS = 4
LANES = 128
SMALL_W = LANES
VMEM_LIMIT = 56 * 1024 * 1024

ROW_TILE = 512
MIX_TILE = 512
FF_TILE = 1024
DN_GROUP = 8
GLA_GROUP = 4

GLA_LEVELS = (32, 16, 8, 4, 2, 1)


def _mm(a, b):
    return jnp.dot(a, b, preferred_element_type=F32)


def _mm_nt(a, b):
    return lax.dot_general(a, b, (((1,), (1,)), ((), ())), preferred_element_type=F32)


def _mm_tn(a, b):
    return lax.dot_general(a, b, (((0,), (0,)), ((), ())), preferred_element_type=F32)


def _rms(x, w):
    return x * lax.rsqrt(jnp.mean(x * x, axis=-1, keepdims=True) + EPS) * w


def _softplus(x):
    return jnp.maximum(x, 0.0) + jnp.log1p(jnp.exp(-jnp.abs(x)))


def _silu(x):
    return x * jax.nn.sigmoid(x)


def _proj_kernel(h_ref, nw_ref, wdn_ref, wgla_ref, wsm_ref, wg2_ref, gpar_ref, bg_ref,
                 dn_ref, gla_ref, gates_ref, loga_ref):
    xb = _rms(h_ref[...], nw_ref[...]).astype(BF16)
    dn_ref[...] = _mm(xb, wdn_ref[...]).astype(BF16)
    gla_ref[...] = _mm(xb, wgla_ref[...]).astype(BF16)
    sm = _mm(xb, wsm_ref[...])
    lane = lax.broadcasted_iota(jnp.int32, sm.shape, 1)
    beta = jax.nn.sigmoid(sm)
    g = -jnp.exp(gpar_ref[0:1, :]) * _softplus(sm + gpar_ref[1:2, :])
    gates_ref[...] = jnp.where(lane < DN_HEADS, beta, g)
    logit = _mm(sm.astype(BF16), wg2_ref[...]) + bg_ref[...]
    loga_ref[...] = -_softplus(-logit) * (1.0 / GLA_TAU)


def _proj_call(h, nw, wdn, wgla, wsm, wg2, gpar, bg, layer):
    t, d = h.shape
    tm = ROW_TILE
    wspec = lambda shape: pl.BlockSpec((None,) + shape, lambda i: (layer, 0, 0))
    return pl.pallas_call(
        _proj_kernel,
        grid=(t // tm,),
        in_specs=[
            pl.BlockSpec((tm, d), lambda i: (i, 0)),
            wspec((1, d)),
            wspec((d, 4 * DN_DIM)),
            wspec((d, 2 * GLA_K_DIM + 2 * GLA_V_DIM)),
            wspec((d, SMALL_W)),
            wspec((SMALL_W, GLA_K_DIM)),
            wspec((8, SMALL_W)),
            wspec((1, GLA_K_DIM)),
        ],
        out_specs=[
            pl.BlockSpec((tm, 4 * DN_DIM), lambda i: (i, 0)),
            pl.BlockSpec((tm, 2 * GLA_K_DIM + 2 * GLA_V_DIM), lambda i: (i, 0)),
            pl.BlockSpec((tm, SMALL_W), lambda i: (i, 0)),
            pl.BlockSpec((tm, GLA_K_DIM), lambda i: (i, 0)),
        ],
        out_shape=[
            jax.ShapeDtypeStruct((t, 4 * DN_DIM), BF16),
            jax.ShapeDtypeStruct((t, 2 * GLA_K_DIM + 2 * GLA_V_DIM), BF16),
            jax.ShapeDtypeStruct((t, SMALL_W), F32),
            jax.ShapeDtypeStruct((t, GLA_K_DIM), F32),
        ],
        compiler_params=pltpu.CompilerParams(
            dimension_semantics=("arbitrary",), vmem_limit_bytes=VMEM_LIMIT),
    )(h, nw, wdn, wgla, wsm, wg2, gpar, bg)


def _dn_kernel(x_ref, gates_ref, cw_ref, nw_ref, tri_ref, o_ref,
               xpad_ref, state_ref, qm_ref, n_ref, o1_ref, gl_ref, obuf_ref, *, tb):
    c3 = 3 * DN_DIM
    hd = DN_HEAD_DIM

    @pl.when(pl.program_id(1) == 0)
    def _():
        state_ref[...] = jnp.zeros_like(state_ref)
        xpad_ref[0:8, :] = jnp.zeros((8, c3), F32)

    xpad_ref[8:8 + tb, :] = x_ref[:, 0:c3].astype(F32)

    row = lax.broadcasted_iota(jnp.int32, (CHUNK, CHUNK), 0)
    col = lax.broadcasted_iota(jnp.int32, (CHUNK, CHUNK), 1)
    causal = row >= col
    strict = row > col

    g = DN_GROUP
    gc_rows = g * CHUNK
    heads = range(DN_HEADS)
    bmm = functools.partial(jnp.einsum, "cij,cjl->cil", preferred_element_type=F32)
    bmm_nt = functools.partial(jnp.einsum, "cik,cjk->cij", preferred_element_type=F32)

    def prepare(gi, carry):
        base = pl.multiple_of(gi * gc_rows, gc_rows)
        cbase = gi * g

        def conv(col):
            win = xpad_ref[pl.ds(base, gc_rows + 8), col:col + hd]
            y = win[8:] * cw_ref[3:4, col:col + hd]
            for w in range(CONV_WIDTH - 1):
                shifted = pltpu.roll(win, CONV_WIDTH - 1 - w, 0)
                y = y + shifted[8:] * cw_ref[w:w + 1, col:col + hd]
            return _silu(y)

        gt = gates_ref[pl.ds(base, gc_rows), :].reshape(g, CHUNK, SMALL_W)
        tri_b = jnp.broadcast_to(tri_ref[...], (g, CHUNK, CHUNK))
        gc = jnp.einsum("cij,cjl->cil", tri_b, gt, precision=lax.Precision.HIGHEST,
                        preferred_element_type=F32)
        gct = jnp.swapaxes(gc, 1, 2)
        q, k, v, gcol, bcol, glast, decay, a, aqk = ([None] * DN_HEADS for _ in range(9))
        for h in heads:
            qh = conv(h * hd)
            kh = conv(DN_DIM + h * hd)
            qh = qh * lax.rsqrt(jnp.sum(qh * qh, axis=-1, keepdims=True) + EPS) * (hd ** -0.5)
            kh = kh * lax.rsqrt(jnp.sum(kh * kh, axis=-1, keepdims=True) + EPS)
            q[h] = qh.reshape(g, CHUNK, hd)
            k[h] = kh.reshape(g, CHUNK, hd)
            v[h] = conv(2 * DN_DIM + h * hd).reshape(g, CHUNK, hd)
            gcol[h] = gc[:, :, DN_HEADS + h:DN_HEADS + h + 1]
            grow = gct[:, DN_HEADS + h:DN_HEADS + h + 1, :]
            bcol[h] = gt[:, :, h:h + 1]
            glast[h] = gc[:, CHUNK - 1:CHUNK, DN_HEADS + h:DN_HEADS + h + 1]
            decay[h] = jnp.exp(jnp.where(causal, gcol[h] - grow, -jnp.inf))
        for h in heads:
            kb = k[h].astype(BF16)
            a[h] = jnp.where(strict, bcol[h] * bmm_nt(kb, kb) * decay[h], 0.0)
            aqk[h] = bmm_nt(q[h].astype(BF16), kb) * decay[h]
        tm = [-a[h] for h in heads]
        xf = [None] * DN_HEADS
        for h in heads:
            ab = a[h].astype(BF16)
            xf[h] = bmm(ab, ab)
        for it in range(5):
            xb = [xf[h].astype(BF16) for h in heads]
            for h in heads:
                tm[h] = tm[h] + xf[h] + bmm(tm[h].astype(BF16), xb[h])
            if it < 4:
                for h in heads:
                    xf[h] = bmm(xb[h], xb[h])
        for h in heads:
            egc = jnp.exp(gcol[h])
            rhs = jnp.concatenate([k[h] * (bcol[h] * egc), v[h] * bcol[h]], axis=2)
            wu = rhs + bmm(tm[h].astype(BF16), rhs.astype(BF16))
            kdt = jnp.swapaxes(k[h] * jnp.exp(glast[h] - gcol[h]), 1, 2)
            r2 = bmm(jnp.concatenate([kdt, aqk[h]], axis=1).astype(BF16), wu.astype(BF16))
            qm_ref[h, pl.ds(cbase, g)] = jnp.concatenate(
                [-r2[:, 0:hd, 0:hd], q[h] * egc - r2[:, hd:hd + CHUNK, 0:hd]], axis=1).astype(BF16)
            n_ref[h, pl.ds(cbase, g)] = r2[:, 0:hd, hd:2 * hd]
            o1_ref[h, pl.ds(cbase, g)] = r2[:, hd:hd + CHUNK, hd:2 * hd]
            gl_ref[h, pl.ds(cbase, g)] = jnp.broadcast_to(jnp.exp(glast[h]), (g, 8, hd))
        return carry

    def recur(c, carry):
        s_old = [state_ref[h] for h in heads]
        r = [_mm(qm_ref[h, c], s_old[h].astype(BF16)) for h in heads]
        for h in heads:
            decayed = (s_old[h].reshape(hd // 8, 8, hd) * gl_ref[h, c][None]).reshape(hd, hd)
            state_ref[h] = decayed + n_ref[h, c] + r[h][0:hd]
            obuf_ref[pl.ds(pl.multiple_of(c * CHUNK, CHUNK), CHUNK), h * hd:(h + 1) * hd] = (
                r[h][hd:hd + CHUNK] + o1_ref[h, c])
        return carry

    lax.fori_loop(0, tb // gc_rows, prepare, 0)
    lax.fori_loop(0, tb // CHUNK, recur, 0)
    for h in heads:
        z = x_ref[:, c3 + h * hd:c3 + (h + 1) * hd].astype(F32)
        o_ref[:, h * hd:(h + 1) * hd] = (
            _rms(obuf_ref[:, h * hd:(h + 1) * hd], nw_ref[...]) * _silu(z)).astype(BF16)
    xpad_ref[0:8, :] = xpad_ref[tb:tb + 8, :]


def _dn_call(dn_in, gates, cw, nw, tri, layer, batch):
    t = dn_in.shape[0]
    tb = MIX_TILE
    nb = t // batch // tb
    nc = tb // CHUNK
    return pl.pallas_call(
        functools.partial(_dn_kernel, tb=tb),
        grid=(batch, nb),
        in_specs=[
            pl.BlockSpec((tb, 4 * DN_DIM), lambda b, j: (b * nb + j, 0)),
            pl.BlockSpec((tb, SMALL_W), lambda b, j: (b * nb + j, 0)),
            pl.BlockSpec((None, 8, 3 * DN_DIM), lambda b, j: (layer, 0, 0)),
            pl.BlockSpec((None, 1, DN_HEAD_DIM), lambda b, j: (layer, 0, 0)),
            pl.BlockSpec((CHUNK, CHUNK), lambda b, j: (0, 0)),
        ],
        out_specs=pl.BlockSpec((tb, DN_DIM), lambda b, j: (b * nb + j, 0)),
        out_shape=jax.ShapeDtypeStruct((t, DN_DIM), BF16),
        scratch_shapes=[
            pltpu.VMEM((tb + 8, 3 * DN_DIM), F32),
            pltpu.VMEM((DN_HEADS, DN_HEAD_DIM, DN_HEAD_DIM), F32),
            pltpu.VMEM((DN_HEADS, nc, DN_HEAD_DIM + CHUNK, DN_HEAD_DIM), BF16),
            pltpu.VMEM((DN_HEADS, nc, DN_HEAD_DIM, DN_HEAD_DIM), F32),
            pltpu.VMEM((DN_HEADS, nc, CHUNK, DN_HEAD_DIM), F32),
            pltpu.VMEM((DN_HEADS, nc, 8, DN_HEAD_DIM), F32),
            pltpu.VMEM((tb, DN_DIM), F32),
        ],
        compiler_params=pltpu.CompilerParams(
            dimension_semantics=("arbitrary", "arbitrary"), vmem_limit_bytes=VMEM_LIMIT),
    )(dn_in, gates, cw, nw, tri)


def _gla_kernel(x_ref, loga_ref, nw_ref, tri_ref, lvl_ref, bmk_ref, bmv_ref, bmvt_ref, o_ref, state_ref,
                *, tb):
    kd_ = GLA_K_DIM
    vd_ = GLA_V_DIM

    @pl.when(pl.program_id(1) == 0)
    def _():
        state_ref[...] = jnp.zeros_like(state_ref)

    g = GLA_GROUP
    rows = g * CHUNK
    rowi = lax.broadcasted_iota(jnp.int32, (rows, kd_), 0)
    bmm = functools.partial(jnp.einsum, "cij,cjl->cil", preferred_element_type=F32)
    bmm_nt = functools.partial(jnp.einsum, "cik,cjk->cij", preferred_element_type=F32)

    def split(a):
        return a.reshape(g, CHUNK, a.shape[-1])

    def blockdiag_k(kt):
        return jnp.concatenate([split(kt.astype(BF16))] * GLA_HEADS, axis=1) * bmk_ref[...][None]

    def group(gi, carry):
        base = pl.multiple_of(gi * rows, rows)
        q = x_ref[pl.ds(base, rows), 0:kd_].astype(F32) * (GLA_HEAD_K ** -0.5)
        k = x_ref[pl.ds(base, rows), kd_:2 * kd_].astype(F32)
        v = split(x_ref[pl.ds(base, rows), 2 * kd_:2 * kd_ + vd_])
        r = x_ref[pl.ds(base, rows), 2 * kd_ + vd_:2 * kd_ + 2 * vd_].astype(F32)
        la = split(loga_ref[pl.ds(base, rows), :])
        tri_b = jnp.broadcast_to(tri_ref[...], (g, CHUNK, CHUNK))
        gc3 = jnp.einsum("cij,cjl->cil", tri_b, la, precision=lax.Precision.HIGHEST,
                         preferred_element_type=F32)
        gc = gc3.reshape(rows, kd_)

        att = lvl_ref[len(GLA_LEVELS)][None] * bmm_nt(split(q.astype(BF16)), blockdiag_k(k))
        for li, s in enumerate(GLA_LEVELS):
            pos = rowi & (2 * s - 1)
            lower = pos >= s
            if 2 * s >= 8:
                n = rows // (2 * s)
                bnd = gc.reshape(n, 2 * s, kd_)[:, s - 1:s, :]
                gb = jnp.broadcast_to(bnd, (n, 2 * s, kd_)).reshape(rows, kd_)
                dq = gc - gb
                dk = gb - gc
            elif s == 2:
                r1 = pltpu.roll(gc, 1, 0)
                r2 = pltpu.roll(gc, 2, 0)
                up1 = pltpu.roll(gc, rows - 1, 0)
                dq = gc - jnp.where(pos == 2, r1, r2)
                dk = jnp.where(pos == 0, up1, gc) - gc
            else:
                dq = gc - pltpu.roll(gc, 1, 0)
                dk = jnp.zeros_like(gc)
            qt = q * jnp.exp(jnp.where(lower, dq, -jnp.inf))
            kt = k * jnp.exp(jnp.where(lower, -jnp.inf, dk))
            att = att + lvl_ref[li][None] * bmm_nt(split(qt.astype(BF16)), blockdiag_k(kt))

        glast = gc3[:, CHUNK - 1:CHUNK, :]
        qg = split((q * jnp.exp(gc)).astype(BF16))
        kdec = (split(k) * jnp.exp(glast - gc3)).astype(BF16)
        vblk = jnp.concatenate([v] * GLA_HEADS, axis=1) * bmv_ref[...][None]
        o_intra = bmm(att.astype(BF16), vblk)
        upd = [_mm_tn(v[c], kdec[c]) * bmvt_ref[...] for c in range(g)]
        st = state_ref[...]
        outs = []
        for c in range(g):
            outs.append(_mm_nt(qg[c], st.astype(BF16)) + o_intra[c])
            st = st * jnp.exp(glast[c]) + upd[c]
        state_ref[...] = st
        o = jnp.concatenate(outs, axis=0)
        normed = [_rms(o[:, h * GLA_HEAD_V:(h + 1) * GLA_HEAD_V], nw_ref[...]) for h in range(GLA_HEADS)]
        o_ref[pl.ds(base, rows), :] = (jnp.concatenate(normed, axis=1) * _silu(r)).astype(BF16)
        return carry

    lax.fori_loop(0, tb // rows, group, 0)


def _gla_call(gla_in, loga, nw, tri, lvl, bmk, bmv, bmvt, layer, batch):
    t = gla_in.shape[0]
    tb = MIX_TILE
    nb = t // batch // tb
    win = 2 * GLA_K_DIM + 2 * GLA_V_DIM
    const2 = lambda shape: pl.BlockSpec(shape, lambda b, j: (0,) * len(shape))
    return pl.pallas_call(
        functools.partial(_gla_kernel, tb=tb),
        grid=(batch, nb),
        in_specs=[
            pl.BlockSpec((tb, win), lambda b, j: (b * nb + j, 0)),
            pl.BlockSpec((tb, GLA_K_DIM), lambda b, j: (b * nb + j, 0)),
            pl.BlockSpec((None, 1, GLA_HEAD_V), lambda b, j: (layer, 0, 0)),
            const2((CHUNK, CHUNK)),
            const2((len(GLA_LEVELS) + 1, CHUNK, GLA_K_DIM)),
            const2((GLA_K_DIM, GLA_K_DIM)),
            const2((GLA_K_DIM, GLA_V_DIM)),
            const2((GLA_V_DIM, GLA_K_DIM)),
        ],
        out_specs=pl.BlockSpec((tb, GLA_V_DIM), lambda b, j: (b * nb + j, 0)),
        out_shape=jax.ShapeDtypeStruct((t, GLA_V_DIM), BF16),
        scratch_shapes=[pltpu.VMEM((GLA_V_DIM, GLA_K_DIM), F32)],
        compiler_params=pltpu.CompilerParams(
            dimension_semantics=("arbitrary", "arbitrary"), vmem_limit_bytes=VMEM_LIMIT),
    )(gla_in, loga, nw, tri, lvl, bmk, bmv, bmvt)


def _gla_constants():
    i = np.arange(CHUNK)[:, None]
    j = np.arange(CHUNK)[None, :]
    masks = []
    for s in GLA_LEVELS:
        same_block = (i // (2 * s)) == (j // (2 * s))
        masks.append(same_block & ((i % (2 * s)) >= s) & ((j % (2 * s)) < s))
    masks.append(i == j)
    lvl = np.stack([np.tile(m, (1, GLA_HEADS)) for m in masks]).astype(np.float32)
    hr = np.arange(GLA_K_DIM)[:, None] // CHUNK
    bmk = (hr == (np.arange(GLA_K_DIM)[None, :] // GLA_HEAD_K)).astype(np.float32)
    bmv = (hr == (np.arange(GLA_V_DIM)[None, :] // GLA_HEAD_V)).astype(np.float32)
    return (jnp.asarray(lvl), jnp.asarray(bmk, dtype=BF16), jnp.asarray(bmv, dtype=BF16),
            jnp.asarray(bmv.T))


def _memkv_kernel(mem_ref, nw_ref, wk_ref, wv_ref, k_ref, v_ref):
    mb = _rms(mem_ref[...], nw_ref[...]).astype(BF16)
    k_ref[...] = _mm(mb, wk_ref[...]).astype(BF16)
    v_ref[...] = _mm(mb, wv_ref[...]).astype(BF16)


def _memkv_call(mem2d, nw, wk, wv, batch):
    tm, d = mem2d.shape
    m = tm // batch
    depth = wk.shape[0]
    return pl.pallas_call(
        _memkv_kernel,
        grid=(depth, batch),
        in_specs=[
            pl.BlockSpec((m, d), lambda l, b: (b, 0)),
            pl.BlockSpec((None, 1, d), lambda l, b: (l, 0, 0)),
            pl.BlockSpec((None, d, d), lambda l, b: (l, 0, 0)),
            pl.BlockSpec((None, d, d), lambda l, b: (l, 0, 0)),
        ],
        out_specs=[
            pl.BlockSpec((None, m, d), lambda l, b: (l, b, 0)),
            pl.BlockSpec((None, m, d), lambda l, b: (l, b, 0)),
        ],
        out_shape=[
            jax.ShapeDtypeStruct((depth, tm, d), BF16),
            jax.ShapeDtypeStruct((depth, tm, d), BF16),
        ],
        compiler_params=pltpu.CompilerParams(
            dimension_semantics=("arbitrary", "arbitrary"), vmem_limit_bytes=VMEM_LIMIT),
    )(mem2d, nw, wk, wv)


def _xa_kernel(h_ref, odn_ref, ogla_ref, wout_ref, nw_ref, wq_ref, k_ref, v_ref, wo_ref, out_ref):
    d = h_ref.shape[1]
    hd = d // XA_HEADS
    h1 = (h_ref[...] + _mm(odn_ref[...], wout_ref[0:DN_DIM, :])
          + _mm(ogla_ref[...], wout_ref[DN_DIM:DN_DIM + GLA_V_DIM, :]))
    xb = _rms(h1, nw_ref[...]).astype(BF16)
    q = _mm(xb, wq_ref[...]).astype(BF16)
    outs = []
    for h in range(XA_HEADS):
        s = _mm_nt(q[:, h * hd:(h + 1) * hd], k_ref[:, h * hd:(h + 1) * hd]) * (hd ** -0.5)
        p = jnp.exp(s - jnp.max(s, axis=-1, keepdims=True))
        p = p / jnp.sum(p, axis=-1, keepdims=True)
        outs.append(_mm(p.astype(BF16), v_ref[:, h * hd:(h + 1) * hd]).astype(BF16))
    out_ref[...] = h1 + _mm(jnp.concatenate(outs, axis=1), wo_ref[...])


def _xa_call(h, odn, ogla, wout, nw, wq, kmem, vmem, wo, layer, batch):
    t, d = h.shape
    tm = ROW_TILE
    nb = t // batch // tm
    m = kmem.shape[1] // batch
    wspec = lambda shape: pl.BlockSpec((None,) + shape, lambda b, j: (layer, 0, 0))
    return pl.pallas_call(
        _xa_kernel,
        grid=(batch, nb),
        in_specs=[
            pl.BlockSpec((tm, d), lambda b, j: (b * nb + j, 0)),
            pl.BlockSpec((tm, DN_DIM), lambda b, j: (b * nb + j, 0)),
            pl.BlockSpec((tm, GLA_V_DIM), lambda b, j: (b * nb + j, 0)),
            wspec((DN_DIM + GLA_V_DIM, d)),
            wspec((1, d)),
            wspec((d, d)),
            pl.BlockSpec((None, m, d), lambda b, j: (layer, b, 0)),
            pl.BlockSpec((None, m, d), lambda b, j: (layer, b, 0)),
            wspec((d, d)),
        ],
        out_specs=pl.BlockSpec((tm, d), lambda b, j: (b * nb + j, 0)),
        out_shape=jax.ShapeDtypeStruct((t, d), F32),
        compiler_params=pltpu.CompilerParams(
            dimension_semantics=("arbitrary", "arbitrary"), vmem_limit_bytes=VMEM_LIMIT),
    )(h, odn, ogla, wout, nw, wq, kmem, vmem, wo)


def _mlp_kernel(h_ref, nw_ref, w1_ref, w2_ref, fw_ref, out_ref, *, final):
    h = h_ref[...]
    xb = _rms(h, nw_ref[...]).astype(BF16)
    acc = h
    for c in range(w1_ref.shape[1] // FF_TILE):
        a = jnp.maximum(_mm(xb, w1_ref[:, c * FF_TILE:(c + 1) * FF_TILE]), 0.0)
        acc = acc + _mm((a * a).astype(BF16), w2_ref[c * FF_TILE:(c + 1) * FF_TILE, :])
    out_ref[...] = _rms(acc, fw_ref[...]) if final else acc


def _mlp_call(h, nw, w1, w2, fw, layer, final):
    t, d = h.shape
    ff = w1.shape[2]
    tm = ROW_TILE
    wspec = lambda shape: pl.BlockSpec((None,) + shape, lambda i: (layer, 0, 0))
    return pl.pallas_call(
        functools.partial(_mlp_kernel, final=final),
        grid=(t // tm,),
        in_specs=[
            pl.BlockSpec((tm, d), lambda i: (i, 0)),
            wspec((1, d)),
            wspec((d, ff)),
            wspec((ff, d)),
            pl.BlockSpec((1, d), lambda i: (0, 0)),
        ],
        out_specs=pl.BlockSpec((tm, d), lambda i: (i, 0)),
        out_shape=jax.ShapeDtypeStruct((t, d), F32),
        compiler_params=pltpu.CompilerParams(
            dimension_semantics=("arbitrary",), vmem_limit_bytes=VMEM_LIMIT),
    )(h, nw, w1, w2, fw)


def kernel(x, mem, mix_norm_w, w_in, conv_w, dn_a_log, dn_dt_bias, dn_norm_w, gla_w_gate2, gla_b_gate,
           gla_norm_w, w_out, xa_norm_w, mem_norm_w, xa_wq, xa_wk, xa_wv, xa_wo, mlp_norm_w, mlp_w1,
           mlp_w2, final_norm_w):
    batch, seq, d = x.shape
    depth = w_in.shape[0]
    t = batch * seq
    assert seq % MIX_TILE == 0 and seq % ROW_TILE == 0 and MIX_TILE % CHUNK == 0

    sizes = (DN_DIM, DN_DIM, DN_DIM, DN_DIM, DN_HEADS, DN_HEADS,
             GLA_K_DIM, GLA_K_DIM, GLA_V_DIM, GLA_V_DIM, GLA_LOWRANK)
    offs = np.concatenate([[0], np.cumsum(sizes)])
    o_db, o_gq, o_glr = int(offs[4]), int(offs[6]), int(offs[10])
    n_small = 2 * DN_HEADS + GLA_LOWRANK
    w_dn = w_in[:, :, 0:o_db].astype(BF16)
    w_gla = w_in[:, :, o_gq:o_glr].astype(BF16)
    w_sm = jnp.concatenate(
        [w_in[:, :, o_db:o_gq], w_in[:, :, o_glr:],
         jnp.zeros((depth, d, SMALL_W - n_small), F32)], axis=2).astype(BF16)
    wg2 = jnp.zeros((depth, SMALL_W, GLA_K_DIM), F32)
    wg2 = wg2.at[:, 2 * DN_HEADS:n_small, :].set(gla_w_gate2).astype(BF16)
    gpar = jnp.zeros((depth, 8, SMALL_W), F32)
    gpar = gpar.at[:, 0, DN_HEADS:2 * DN_HEADS].set(dn_a_log)
    gpar = gpar.at[:, 1, DN_HEADS:2 * DN_HEADS].set(dn_dt_bias)
    bg = gla_b_gate.reshape(depth, 1, GLA_K_DIM)
    cw = jnp.zeros((depth, 8, 3 * DN_DIM), F32).at[:, 0:CONV_WIDTH, :].set(conv_w)
    dn_nw = dn_norm_w.reshape(depth, 1, DN_HEAD_DIM)
    gla_nw = gla_norm_w.reshape(depth, 1, GLA_HEAD_V)
    tri = jnp.asarray(np.tril(np.ones((CHUNK, CHUNK), np.float32)))
    lvl, bmk, bmv, bmvt = _gla_constants()

    mix_nw = mix_norm_w.reshape(depth, 1, d)
    xa_nw = xa_norm_w.reshape(depth, 1, d)
    mem_nw = mem_norm_w.reshape(depth, 1, d)
    mlp_nw = mlp_norm_w.reshape(depth, 1, d)
    wout_b = w_out.astype(BF16)
    wq_b = xa_wq.astype(BF16)
    wo_b = xa_wo.astype(BF16)
    w1_b = mlp_w1.astype(BF16)
    w2_b = mlp_w2.astype(BF16)
    fw = final_norm_w.reshape(1, d)

    kmem, vmem = _memkv_call(mem.reshape(batch * mem.shape[1], d), mem_nw,
                             xa_wk.astype(BF16), xa_wv.astype(BF16), batch)

    h = x.reshape(t, d)
    for l in range(depth):
        dn_in, gla_in, gates, loga = _proj_call(h, mix_nw, w_dn, w_gla, w_sm, wg2, gpar, bg, l)
        o_dn = _dn_call(dn_in, gates, cw, dn_nw, tri, l, batch)
        o_gla = _gla_call(gla_in, loga, gla_nw, tri, lvl, bmk, bmv, bmvt, l, batch)
        h = _xa_call(h, o_dn, o_gla, wout_b, xa_nw, wq_b, kmem, vmem, wo_b, l, batch)
        h = _mlp_call(h, mlp_nw, w1_b, w2_b, fw, l, l == depth - 1)
    return h.reshape(batch, seq, d)
```
